```python
import math
import jax
import jax.numpy as jnp
from jax import lax
import numpy as np

D_MODEL = 1024
BATCH = 16
SEQ = 2048
DEPTH = 2

ROPE_THETA = 10000.0
NORM_EPS = 1e-6
Q_BLOCK = 128

MOBA_HEADS = 8
MOBA_HEAD_DIM = 64
MOBA_WIDTH = MOBA_HEADS * MOBA_HEAD_DIM
MOBA_BLOCK = 256
MOBA_TOPK = 3

SSD_HEADS = 8
SSD_HEAD_DIM = 64
SSD_INNER = SSD_HEADS * SSD_HEAD_DIM
SSD_GROUPS = 2
SSD_STATE = 128
SSD_CONV = 4
SSD_CHUNK = 128
SSD_XBC = SSD_INNER + 2 * SSD_GROUPS * SSD_STATE

EVEN_IN = 3 * MOBA_WIDTH + SSD_INNER + SSD_XBC + SSD_HEADS
EVEN_MIX = MOBA_WIDTH + SSD_INNER

MLA_HEADS = 16
MLA_NOPE = 64
MLA_ROPE = 32
MLA_QK = MLA_NOPE + MLA_ROPE
MLA_V = 64
MLA_Q_RANK = 512
MLA_KV_RANK = 256
MLA_IN = MLA_Q_RANK + MLA_KV_RANK + MLA_ROPE

D_FF = 2816
FFN_CONV = 3

kernel_name = "hybrid_moba_ssd_mla_convffn"


def rms_norm(x, g):
    xf = x.astype(jnp.float32)
    y = xf * lax.rsqrt(jnp.mean(xf * xf, axis=-1, keepdims=True) + NORM_EPS)
    return (y * g.astype(jnp.float32)).astype(x.dtype)


def apply_rope(x):
    d = x.shape[-1]
    inv_freq = 1.0 / (ROPE_THETA ** (jnp.arange(0, d, 2, dtype=jnp.float32) / d))
    ang = jnp.arange(x.shape[1], dtype=jnp.float32)[:, None] * inv_freq[None, :]
    cos = jnp.cos(ang)[None, :, None, :]
    sin = jnp.sin(ang)[None, :, None, :]
    xf = x.astype(jnp.float32)
    x1, x2 = xf[..., : d // 2], xf[..., d // 2:]
    return jnp.concatenate([x1 * cos - x2 * sin, x1 * sin + x2 * cos], axis=-1).astype(x.dtype)


def causal_dwconv(x, w, b):
    K = w.shape[0]
    S = x.shape[1]
    xp = jnp.pad(x, ((0, 0), (K - 1, 0), (0, 0)))
    y = b
    for k in range(K):
        y = y + xp[:, k:k + S] * w[k]
    return y


def moba_attention(q, k, v):
    bsz, S, H, dh = q.shape
    scale = dh ** -0.5
    n_blk = -(-S // MOBA_BLOCK)
    pad = n_blk * MOBA_BLOCK - S
    n_sel = min(MOBA_TOPK, n_blk)
    nqb = S // Q_BLOCK
    padw = ((0, 0), (0, pad), (0, 0), (0, 0))
    kb = jnp.pad(k, padw).reshape(bsz, n_blk, MOBA_BLOCK, H, dh).transpose(0, 3, 1, 2, 4)
    vb = jnp.pad(v, padw).reshape(bsz, n_blk, MOBA_BLOCK, H, dh).transpose(0, 3, 1, 2, 4)
    qt = q.transpose(0, 2, 1, 3)
    k_mean = jnp.mean(kb.astype(jnp.float32), axis=3)
    gate = jnp.einsum("bhsd,bhnd->bhsn", qt.astype(jnp.float32), k_mean)
    own = jnp.arange(S) // MOBA_BLOCK
    fully_past = jnp.arange(n_blk)[None, :] < own[:, None]
    gate = jnp.where(fully_past, gate, -jnp.inf)
    _, sel = lax.top_k(gate, n_sel)
    sel_ok = sel < own[None, None, :, None]

    def to_qblocks(t):
        t = t.reshape(bsz, H, nqb, Q_BLOCK, t.shape[-1]).transpose(0, 2, 1, 3, 4)
        return t.reshape(bsz * nqb, H, Q_BLOCK, t.shape[-1])

    b_ids = jnp.repeat(jnp.arange(bsz), nqb)
    j_ids = jnp.tile(jnp.arange(nqb), bsz)
    heads = jnp.arange(H)[:, None, None]
    blk_pos = jnp.arange(MOBA_BLOCK)
    q_off = jnp.arange(Q_BLOCK)

    def one_block(args):
        qi, idx, ok, b, j = args
        k_b = kb[b]
        v_b = vb[b]
        k_sel = k_b[heads, idx]
        v_sel = v_b[heads, idx]
        own_j = (j * Q_BLOCK) // MOBA_BLOCK
        k_own = lax.dynamic_index_in_dim(k_b, own_j, axis=1, keepdims=False)
        v_own = lax.dynamic_index_in_dim(v_b, own_j, axis=1, keepdims=False)
        s_sel = jnp.einsum("hqd,hqnld->hqnl", qi, k_sel, preferred_element_type=jnp.float32) * scale
        s_sel = jnp.where(ok[..., None], s_sel, -jnp.inf).reshape(H, Q_BLOCK, n_sel * MOBA_BLOCK)
        s_own = jnp.einsum("hqd,hld->hql", qi, k_own, preferred_element_type=jnp.float32) * scale
        causal = (own_j * MOBA_BLOCK + blk_pos)[None, :] <= (j * Q_BLOCK + q_off)[:, None]
        s_own = jnp.where(causal[None], s_own, -jnp.inf)
        p = jax.nn.softmax(jnp.concatenate([s_sel, s_own], axis=-1), axis=-1).astype(v.dtype)
        p_sel = p[..., : n_sel * MOBA_BLOCK].reshape(H, Q_BLOCK, n_sel, MOBA_BLOCK)
        p_own = p[..., n_sel * MOBA_BLOCK:]
        return jnp.einsum("hqnl,hqnld->hqd", p_sel, v_sel) + jnp.einsum("hql,hld->hqd", p_own, v_own)

    out = lax.map(one_block, (to_qblocks(qt), to_qblocks(sel), to_qblocks(sel_ok), b_ids, j_ids))
    out = out.reshape(bsz, nqb, H, Q_BLOCK, dh).transpose(0, 1, 3, 2, 4)
    return out.reshape(bsz, S, H, dh)


def segsum(a):
    T = a.shape[-1]
    rep = jnp.broadcast_to(a[..., :, None], a.shape + (T,))
    rep = jnp.where(jnp.tril(jnp.ones((T, T), dtype=bool), -1), rep, 0.0)
    ss = jnp.cumsum(rep, axis=-2)
    return jnp.where(jnp.tril(jnp.ones((T, T), dtype=bool)), ss, -jnp.inf)


def ssd_chunked(x, dt, A, Bm, Cm):
    bsz, S, H, P = x.shape
    G, N = Bm.shape[-2], Bm.shape[-1]
    E = H // G
    Q = SSD_CHUNK
    nc = S // Q
    xd = (x * dt[..., None]).reshape(bsz, nc, Q, G, E, P)
    a = (dt * A).reshape(bsz, nc, Q, G, E).transpose(0, 3, 4, 1, 2)
    Bc = Bm.reshape(bsz, nc, Q, G, N)
    Cc = Cm.reshape(bsz, nc, Q, G, N)
    a_cs = jnp.cumsum(a, axis=-1)
    L = jnp.exp(segsum(a))
    CB = jnp.einsum("bclgn,bcsgn->bgcls", Cc, Bc)
    y_diag = jnp.einsum("bgecls,bcsgep->bclgep", CB[:, :, None] * L, xd)
    decay_to_end = jnp.exp(a_cs[..., -1:] - a_cs).transpose(0, 3, 4, 1, 2)
    states = jnp.einsum("bclgn,bclgep->bcgepn", Bc, xd * decay_to_end[..., None])
    chunk_decay = jnp.exp(a_cs[..., -1]).transpose(3, 0, 1, 2)

    def carry_state(h, inp):
        s_c, d_c = inp
        return h * d_c[..., None, None] + s_c, h

    h0 = jnp.zeros((bsz, G, E, P, N), x.dtype)
    _, h_in = lax.scan(carry_state, h0, (states.transpose(1, 0, 2, 3, 4, 5), chunk_decay))
    decay_from_start = jnp.exp(a_cs).transpose(0, 3, 4, 1, 2)
    y_off = jnp.einsum("bclgn,cbgepn->bclgep", Cc, h_in) * decay_from_start[..., None]
    return (y_diag + y_off).reshape(bsz, S, H, P)


def moba_ssd_mixer(h, w_in, conv_w, conv_b, dt_bias, a_log, d_skip, ssd_norm, q_norm, k_norm, w_out):
    bsz, S, _ = h.shape
    proj = h @ w_in
    cuts = [MOBA_WIDTH, 2 * MOBA_WIDTH, 3 * MOBA_WIDTH, 3 * MOBA_WIDTH + SSD_INNER,
            3 * MOBA_WIDTH + SSD_INNER + SSD_XBC]
    q, k, v, z, xbc, dt = jnp.split(proj, cuts, axis=-1)
    hs = (bsz, S, MOBA_HEADS, MOBA_HEAD_DIM)
    q = apply_rope(rms_norm(q.reshape(hs), q_norm))
    k = apply_rope(rms_norm(k.reshape(hs), k_norm))
    o_attn = moba_attention(q, k, v.reshape(hs)).reshape(bsz, S, MOBA_WIDTH)
    xbc = jax.nn.silu(causal_dwconv(xbc, conv_w, conv_b))
    xs, Bm, Cm = jnp.split(xbc, [SSD_INNER, SSD_INNER + SSD_GROUPS * SSD_STATE], axis=-1)
    f32 = jnp.float32
    xs = xs.reshape(bsz, S, SSD_HEADS, SSD_HEAD_DIM).astype(f32)
    dt = jax.nn.softplus(dt.astype(f32) + dt_bias.astype(f32))
    A = -jnp.exp(a_log.astype(f32))
    y = ssd_chunked(xs, dt, A,
                    Bm.reshape(bsz, S, SSD_GROUPS, SSD_STATE).astype(f32),
                    Cm.reshape(bsz, S, SSD_GROUPS, SSD_STATE).astype(f32))
    y = y + d_skip.astype(f32)[:, None] * xs
    y = y.reshape(bsz, S, SSD_INNER).astype(h.dtype) * jax.nn.silu(z)
    gs = SSD_INNER // SSD_GROUPS
    y = rms_norm(y.reshape(bsz, S, SSD_GROUPS, gs), ssd_norm.reshape(SSD_GROUPS, gs)).reshape(bsz, S, SSD_INNER)
    return jnp.concatenate([o_attn, y], axis=-1) @ w_out


def causal_attention(q, k, v):
    bsz, S, H, dq = q.shape
    dv = v.shape[-1]
    scale = dq ** -0.5
    nqb = S // Q_BLOCK
    qb = q.reshape(bsz, nqb, Q_BLOCK, H, dq).transpose(1, 0, 2, 3, 4)
    kpos = jnp.arange(S)
    q_off = jnp.arange(Q_BLOCK)

    def one_block(args):
        qi, j = args
        s = jnp.einsum("bqhd,bkhd->bhqk", qi, k, preferred_element_type=jnp.float32) * scale
        mask = kpos[None, :] <= (j * Q_BLOCK + q_off)[:, None]
        p = jax.nn.softmax(jnp.where(mask, s, -jnp.inf), axis=-1).astype(v.dtype)
        return jnp.einsum("bhqk,bkhd->bqhd", p, v)

    o = lax.map(one_block, (qb, jnp.arange(nqb)))
    return o.transpose(1, 0, 2, 3, 4).reshape(bsz, S, H, dv)


def mla_mixer(h, w_in, q_a_norm, w_uq, kv_a_norm, w_ukv, q_norm, k_norm, w_out):
    bsz, S, _ = h.shape
    c = h @ w_in
    cq, ckv, k_pe = jnp.split(c, [MLA_Q_RANK, MLA_Q_RANK + MLA_KV_RANK], axis=-1)
    q = (rms_norm(cq, q_a_norm) @ w_uq).reshape(bsz, S, MLA_HEADS, MLA_QK)
    kv = (rms_norm(ckv, kv_a_norm) @ w_ukv).reshape(bsz, S, MLA_HEADS, MLA_NOPE + MLA_V)
    k_nope, v = kv[..., :MLA_NOPE], kv[..., MLA_NOPE:]
    k_pe = jnp.broadcast_to(k_pe[:, :, None, :], (bsz, S, MLA_HEADS, MLA_ROPE))
    k = jnp.concatenate([k_nope, k_pe], axis=-1)
    q = rms_norm(q, q_norm)
    k = rms_norm(k, k_norm)
    q = jnp.concatenate([q[..., :MLA_NOPE], apply_rope(q[..., MLA_NOPE:])], axis=-1)
    k = jnp.concatenate([k[..., :MLA_NOPE], apply_rope(k[..., MLA_NOPE:])], axis=-1)
    o = causal_attention(q, k, v)
    return o.reshape(bsz, S, MLA_HEADS * MLA_V) @ w_out


def conv_ffn(h, w_up, conv_w, conv_b, w_down):
    u = causal_dwconv(h @ w_up, conv_w, conv_b)
    g, u = jnp.split(u, 2, axis=-1)
    return (jax.nn.silu(g) * u) @ w_down


def setup_inputs(seed: int = 0) -> dict:
    key = jax.random.key(seed)
    ks = iter(jax.random.split(key, 32))
    f32 = jnp.float32
    n_even = (DEPTH + 1) // 2
    n_odd = DEPTH // 2

    def w(shape, fan_in):
        return jax.random.normal(next(ks), shape, f32) * fan_in ** -0.5

    def gain(shape):
        return 1.0 + 0.05 * jax.random.normal(next(ks), shape, f32)

    def bias(shape):
        return 0.02 * jax.random.normal(next(ks), shape, f32)

    x = jax.random.normal(next(ks), (BATCH, SEQ, D_MODEL), f32)
    mix_norm = gain((DEPTH, D_MODEL))
    ffn_norm = gain((DEPTH, D_MODEL))
    ev_w_in = w((n_even, D_MODEL, EVEN_IN), D_MODEL)
    ev_conv_w = w((n_even, SSD_CONV, SSD_XBC), SSD_CONV)
    ev_conv_b = bias((n_even, SSD_XBC))
    dt0 = jnp.exp(jax.random.uniform(next(ks), (n_even, SSD_HEADS), f32,
                                     minval=math.log(1e-3), maxval=math.log(1e-1)))
    ev_dt_bias = dt0 + jnp.log(-jnp.expm1(-dt0))
    ev_a_log = jnp.log(jax.random.uniform(next(ks), (n_even, SSD_HEADS), f32, minval=1.0, maxval=16.0))
    ev_d_skip = gain((n_even, SSD_HEADS))
    ev_ssd_norm = gain((n_even, SSD_INNER))
    ev_q_norm = gain((n_even, MOBA_HEAD_DIM))
    ev_k_norm = gain((n_even, MOBA_HEAD_DIM))
    ev_w_out = w((n_even, EVEN_MIX, D_MODEL), EVEN_MIX)
    od_w_in = w((n_odd, D_MODEL, MLA_IN), D_MODEL)
    od_q_a_norm = gain((n_odd, MLA_Q_RANK))
    od_w_uq = w((n_odd, MLA_Q_RANK, MLA_HEADS * MLA_QK), MLA_Q_RANK)
    od_kv_a_norm = gain((n_odd, MLA_KV_RANK))
    od_w_ukv = w((n_odd, MLA_KV_RANK, MLA_HEADS * (MLA_NOPE + MLA_V)), MLA_KV_RANK)
    od_q_norm = gain((n_odd, MLA_QK))
    od_k_norm = gain((n_odd, MLA_QK))
    od_w_out = w((n_odd, MLA_HEADS * MLA_V, D_MODEL), MLA_HEADS * MLA_V)
    ffn_w_up = w((DEPTH, D_MODEL, 2 * D_FF), D_MODEL)
    ffn_conv_w = w((DEPTH, FFN_CONV, 2 * D_FF), FFN_CONV)
    ffn_conv_b = bias((DEPTH, 2 * D_FF))
    ffn_w_down = w((DEPTH, D_FF, D_MODEL), D_FF)
    return {
        "x": x, "mix_norm": mix_norm, "ffn_norm": ffn_norm,
        "ev_w_in": ev_w_in, "ev_conv_w": ev_conv_w, "ev_conv_b": ev_conv_b,
        "ev_dt_bias": ev_dt_bias, "ev_a_log": ev_a_log, "ev_d_skip": ev_d_skip,
        "ev_ssd_norm": ev_ssd_norm, "ev_q_norm": ev_q_norm, "ev_k_norm": ev_k_norm,
        "ev_w_out": ev_w_out,
        "od_w_in": od_w_in, "od_q_a_norm": od_q_a_norm, "od_w_uq": od_w_uq,
        "od_kv_a_norm": od_kv_a_norm, "od_w_ukv": od_w_ukv, "od_q_norm": od_q_norm,
        "od_k_norm": od_k_norm, "od_w_out": od_w_out,
        "ffn_w_up": ffn_w_up, "ffn_conv_w": ffn_conv_w, "ffn_conv_b": ffn_conv_b,
        "ffn_w_down": ffn_w_down,
    }


def reference(x, mix_norm, ffn_norm,
              ev_w_in, ev_conv_w, ev_conv_b, ev_dt_bias, ev_a_log, ev_d_skip,
              ev_ssd_norm, ev_q_norm, ev_k_norm, ev_w_out,
              od_w_in, od_q_a_norm, od_w_uq, od_kv_a_norm, od_w_ukv, od_q_norm,
              od_k_norm, od_w_out,
              ffn_w_up, ffn_conv_w, ffn_conv_b, ffn_w_down):
    for layer in range(DEPTH):
        i = layer // 2
        hn = rms_norm(x, mix_norm[layer])
        if layer % 2 == 0:
            x = x + moba_ssd_mixer(hn, ev_w_in[i], ev_conv_w[i], ev_conv_b[i], ev_dt_bias[i],
                                   ev_a_log[i], ev_d_skip[i], ev_ssd_norm[i], ev_q_norm[i],
                                   ev_k_norm[i], ev_w_out[i])
        else:
            x = x + mla_mixer(hn, od_w_in[i], od_q_a_norm[i], od_w_uq[i], od_kv_a_norm[i],
                              od_w_ukv[i], od_q_norm[i], od_k_norm[i], od_w_out[i])
        x = x + conv_ffn(rms_norm(x, ffn_norm[layer]), ffn_w_up[layer], ffn_conv_w[layer],
                         ffn_conv_b[layer], ffn_w_down[layer])
    return x
```

```python
import functools
import math

import jax
import jax.numpy as jnp
from jax import lax
from jax.experimental import pallas as pl
from jax.experimental.pallas import tpu as pltpu

F32 = jnp.float32
BF16 = jnp.bfloat16

D_MODEL = 1024
ROPE_THETA = 10000.0
NORM_EPS = 1e-6

MOBA_HEADS = 8
MOBA_HEAD_DIM = 64
MOBA_WIDTH = MOBA_HEADS * MOBA_HEAD_DIM
MOBA_BLOCK = 256
MOBA_TOPK = 3

SSD_HEADS = 8
SSD_HEAD_DIM = 64
SSD_INNER = SSD_HEADS * SSD_HEAD_DIM
SSD_GROUPS = 2
SSD_STATE = 128
SSD_CONV = 4
SSD_CHUNK = 128
SSD_XBC = SSD_INNER + 2 * SSD_GROUPS * SSD_STATE

MLA_HEADS = 16
MLA_NOPE = 64
MLA_ROPE = 32
MLA_QK = MLA_NOPE + MLA_ROPE
MLA_V = 64
MLA_Q_RANK = 512
MLA_KV_RANK = 256

D_FF = 2816
FFN_CONV = 3

LANES = 128
SUBLANES = 8
ATTN_BLOCK = 256
MASK_VALUE = -1e30
VMEM_LIMIT = 56 * 1024 * 1024

_NT = (((1,), (1,)), ((), ()))


def _cparams(*sem):
    return pltpu.CompilerParams(dimension_semantics=sem, vmem_limit_bytes=VMEM_LIMIT)


def _rms(x, g, n):
    ms = jnp.sum(x * x, axis=-1, keepdims=True) * (1.0 / n)
    return x * lax.rsqrt(ms + NORM_EPS) * g


def _rope(x, cos, sin_signed, split_lane, half):
    lane = lax.broadcasted_iota(jnp.int32, x.shape, 1)
    partner = jnp.where(lane < split_lane, pltpu.roll(x, LANES - half, 1), pltpu.roll(x, half, 1))
    return x * cos + partner * sin_signed


def _silu(x):
    return x * (1.0 / (1.0 + jnp.exp(-x)))


def _inproj0_kernel(x_ref, g_ref, w_ref, qg_ref, kg_ref, cos_ref, sin_ref,
                    q_ref, k_ref, v_ref, z_ref, xbc_ref, dt_ref, km_ref, *, nblk):
    r = pl.program_id(0)
    i = r % nblk

    @pl.when(r == 0)
    def _():
        km_ref[...] = jnp.zeros_like(km_ref)

    hn = _rms(x_ref[...], g_ref[...], D_MODEL).astype(BF16)
    proj = jnp.dot(hn, w_ref[...], preferred_element_type=F32)
    cos = cos_ref[...]
    sin = sin_ref[...]
    tm = hn.shape[0]
    lane = lax.broadcasted_iota(jnp.int32, (tm, LANES), 1)
    nidx = lane - MOBA_HEAD_DIM
    valid = (nidx >= 0) & (nidx < nblk)
    scale = MOBA_HEAD_DIM ** -0.5
    koff = MOBA_HEADS * LANES
    for h in range(MOBA_HEADS):
        qh = proj[:, h * LANES:(h + 1) * LANES]
        kh = proj[:, koff + h * LANES:koff + (h + 1) * LANES]
        qr = _rope(_rms(qh, qg_ref[...], MOBA_HEAD_DIM), cos, sin, MOBA_HEAD_DIM // 2, MOBA_HEAD_DIM // 2)
        kr = _rope(_rms(kh, kg_ref[...], MOBA_HEAD_DIM), cos, sin, MOBA_HEAD_DIM // 2, MOBA_HEAD_DIM // 2)
        gate = lax.dot_general(qr, km_ref[h], _NT, precision=lax.Precision.HIGHEST,
                               preferred_element_type=F32)
        km_ref[h, pl.ds(MOBA_HEAD_DIM + i, 1), :] = jnp.mean(kr, axis=0, keepdims=True)
        ahead = jnp.zeros((tm, LANES), jnp.int32)
        for j in range(1, nblk):
            lower = pltpu.roll(gate, j, 1)
            upper = pltpu.roll(gate, LANES - j, 1)
            ahead += ((nidx >= j) & (lower >= gate)).astype(jnp.int32)
            ahead += ((nidx + j < i) & (upper > gate)).astype(jnp.int32)
        keep = ((nidx < i) & (ahead < MOBA_TOPK)) | (nidx == i)
        bias = jnp.where(valid, jnp.where(keep, 0.0, MASK_VALUE), 0.0)
        q_aug = jnp.where(lane < MOBA_HEAD_DIM, qr * scale, bias)
        k_aug = jnp.where(lane < MOBA_HEAD_DIM, kr, jnp.where(nidx == i, 1.0, 0.0))
        q_ref[:, h * LANES:(h + 1) * LANES] = q_aug.astype(BF16)
        k_ref[:, h * LANES:(h + 1) * LANES] = k_aug.astype(BF16)
    off = 2 * koff
    v_ref[...] = proj[:, off:off + MOBA_WIDTH].astype(BF16)
    off += MOBA_WIDTH
    z_ref[...] = proj[:, off:off + SSD_INNER]
    off += SSD_INNER
    xbc_ref[...] = proj[:, off:off + SSD_XBC]
    off += SSD_XBC
    dt_ref[...] = proj[:, off:off + LANES]


def _inproj0(x, g, w, qg, kg, cos, sin, seq):
    t = x.shape[0]
    tm = MOBA_BLOCK
    nblk = seq // tm
    ncol = w.shape[1]
    row = lambda width: pl.BlockSpec((tm, width), lambda r: (r, 0))
    full = lambda a: pl.BlockSpec(a.shape, lambda r: (0,) * a.ndim)
    tab = pl.BlockSpec((tm, LANES), lambda r: (r % nblk, 0))
    return pl.pallas_call(
        functools.partial(_inproj0_kernel, nblk=nblk),
        grid=(t // tm,),
        in_specs=[row(D_MODEL), full(g), full(w), full(qg), full(kg), tab, tab],
        out_specs=[row(MOBA_HEADS * LANES), row(MOBA_HEADS * LANES), row(MOBA_WIDTH), row(SSD_INNER),
                   row(SSD_XBC), row(LANES)],
        out_shape=[jax.ShapeDtypeStruct((t, MOBA_HEADS * LANES), BF16),
                   jax.ShapeDtypeStruct((t, MOBA_HEADS * LANES), BF16),
                   jax.ShapeDtypeStruct((t, MOBA_WIDTH), BF16),
                   jax.ShapeDtypeStruct((t, SSD_INNER), F32),
                   jax.ShapeDtypeStruct((t, SSD_XBC), F32),
                   jax.ShapeDtypeStruct((t, LANES), F32)],
        scratch_shapes=[pltpu.VMEM((MOBA_HEADS, LANES, LANES), F32)],
        compiler_params=_cparams("arbitrary"),
        name="inproj0",
    )(x, g, w, qg, kg, cos, sin)


def _attn_kernel(q_ref, k_ref, v_ref, o_ref):
    i = pl.program_id(2)
    blk = ATTN_BLOCK
    row = lax.broadcasted_iota(jnp.int32, (blk, blk), 0)
    col = lax.broadcasted_iota(jnp.int32, (blk, blk), 1)
    outs = []
    for j in range(2):
        q = q_ref[0, :, j * LANES:(j + 1) * LANES]

        def scores(n):
            start = pl.multiple_of(n * blk, blk)
            kn = k_ref[0, pl.ds(start, blk), j * LANES:(j + 1) * LANES]
            vn = v_ref[0, pl.ds(start, blk), :]
            return lax.dot_general(q, kn, _NT, preferred_element_type=F32), vn

        s, vd = scores(i)
        s = jnp.where(col <= row, s, MASK_VALUE)
        m = jnp.max(s, axis=-1, keepdims=True)
        p = jnp.exp(s - m)
        l = jnp.sum(p, axis=-1, keepdims=True)
        acc = jnp.dot(p.astype(BF16), vd, preferred_element_type=F32)

        def body(n, carry):
            m, l, acc = carry
            s, vn = scores(n)
            m_new = jnp.maximum(m, jnp.max(s, axis=-1, keepdims=True))
            alpha = jnp.exp(m - m_new)
            p = jnp.exp(s - m_new)
            l = alpha * l + jnp.sum(p, axis=-1, keepdims=True)
            acc = alpha * acc + jnp.dot(p.astype(BF16), vn, preferred_element_type=F32)
            return m_new, l, acc

        m, l, acc = lax.fori_loop(0, i, body, (m, l, acc))
        outs.append(acc / l)
    lane = lax.broadcasted_iota(jnp.int32, (blk, LANES), 1)
    o_ref[0] = jnp.where(lane < LANES // 2, outs[0], outs[1]).astype(o_ref.dtype)


def _attention(q, k, v):
    b, s, _ = q.shape
    pairs = v.shape[-1] // LANES
    nq = s // ATTN_BLOCK
    return pl.pallas_call(
        _attn_kernel,
        grid=(b, pairs, nq),
        in_specs=[pl.BlockSpec((1, ATTN_BLOCK, 2 * LANES), lambda b_, p, i: (b_, i, p)),
                  pl.BlockSpec((1, s, 2 * LANES), lambda b_, p, i: (b_, 0, p)),
                  pl.BlockSpec((1, s, LANES), lambda b_, p, i: (b_, 0, p))],
        out_specs=pl.BlockSpec((1, ATTN_BLOCK, LANES), lambda b_, p, i: (b_, i, p)),
        out_shape=jax.ShapeDtypeStruct((b, s, pairs * LANES), BF16),
        compiler_params=_cparams("arbitrary", "arbitrary", "arbitrary"),
        name="attention",
    )(q, k, v)


def _ssd_kernel(xbc_ref, z_ref, dt_ref, cw_ref, cb_ref, dtb_ref, alog_ref, dskip_ref, gn_ref,
                y_ref, prev_ref, state_ref):
    c = pl.program_id(1)
    q = SSD_CHUNK

    @pl.when(c == 0)
    def _():
        prev_ref[...] = jnp.zeros_like(prev_ref)
        state_ref[...] = jnp.zeros_like(state_ref)

    cur = xbc_ref[...]
    prv = prev_ref[...]
    cw = cw_ref[...]
    rowx = lax.broadcasted_iota(jnp.int32, cur.shape, 0)
    conv = cb_ref[...] + cw[SSD_CONV - 1:SSD_CONV] * cur
    for k in range(1, SSD_CONV):
        shifted = pltpu.roll(jnp.where(rowx >= q - k, prv, cur), k, 0)
        conv = conv + cw[SSD_CONV - 1 - k:SSD_CONV - k] * shifted
    prev_ref[...] = cur
    act = _silu(conv)
    xs = act[:, :SSD_INNER]
    bm = act[:, SSD_INNER:SSD_INNER + SSD_GROUPS * SSD_STATE]
    cm = act[:, SSD_INNER + SSD_GROUPS * SSD_STATE:]

    dtr = dt_ref[...] + dtb_ref[...]
    dt = jnp.maximum(dtr, 0.0) + jnp.log1p(jnp.exp(-jnp.abs(dtr)))
    a = dt * (-jnp.exp(alog_ref[...]))
    row = lax.broadcasted_iota(jnp.int32, (q, LANES), 0)
    col = lax.broadcasted_iota(jnp.int32, (q, LANES), 1)
    a_cs = a
    d = 1
    while d < q:
        a_cs = a_cs + jnp.where(row >= d, pltpu.roll(a_cs, d, 0), 0.0)
        d *= 2
    a_cs_t = a_cs.T
    a_last = a_cs[q - 1:q, :]
    e_start = jnp.exp(a_cs)
    e_end = jnp.exp(a_last - a_cs)
    causal = col <= row
    lo = col < LANES // 2

    def pair_cols(mat, h0):
        return jnp.where(lo, mat[:, h0:h0 + 1], mat[:, h0 + 1:h0 + 2])

    ys = []
    for g in range(SSD_GROUPS):
        b_g = bm[:, g * SSD_STATE:(g + 1) * SSD_STATE]
        c_g = cm[:, g * SSD_STATE:(g + 1) * SSD_STATE]
        c_bf = c_g.astype(BF16)
        bt_bf = b_g.T.astype(BF16)
        cb = lax.dot_general(c_bf, b_g.astype(BF16), _NT, preferred_element_type=F32)
        pairs_per_group = SSD_HEADS // SSD_GROUPS // 2
        for pp in range(pairs_per_group):
            pr = g * pairs_per_group + pp
            h0 = 2 * pr
            xs_p = xs[:, pr * LANES:(pr + 1) * LANES]
            xd = xs_p * pair_cols(dt, h0)
            xd_bf = xd.astype(BF16)
            halves = []
            for h in (h0, h0 + 1):
                seg = a_cs[:, h:h + 1] - a_cs_t[h:h + 1, :]
                decay = jnp.exp(jnp.where(causal, seg, MASK_VALUE))
                halves.append(jnp.dot((cb * decay).astype(BF16), xd_bf, preferred_element_type=F32))
            y_diag = jnp.where(lo, halves[0], halves[1])
            e_start_p = pair_cols(e_start, h0)
            h_prev = state_ref[pr]
            y_off = jnp.dot(c_bf, h_prev.astype(BF16), preferred_element_type=F32) * e_start_p
            upd = jnp.dot(bt_bf, (xd * pair_cols(e_end, h0)).astype(BF16), preferred_element_type=F32)
            state_ref[pr] = h_prev * e_start_p[q - 1:q, :] + upd
            ys.append(y_diag + y_off + dskip_ref[:, pr * LANES:(pr + 1) * LANES] * xs_p)
    y = jnp.concatenate(ys, axis=-1) * _silu(z_ref[...])
    gs = SSD_INNER // SSD_GROUPS
    gn = gn_ref[...]
    outs = [_rms(y[:, g * gs:(g + 1) * gs], gn[:, g * gs:(g + 1) * gs], gs) for g in range(SSD_GROUPS)]
    y_ref[...] = jnp.concatenate(outs, axis=-1).astype(y_ref.dtype)


def _ssd(xbc, z, dt, cw, cb, dtb, alog, dskip, gn, seq):
    t = xbc.shape[0]
    q = SSD_CHUNK
    nc = seq // q
    row = lambda width: pl.BlockSpec((q, width), lambda b, c: (b * nc + c, 0))
    full = lambda a: pl.BlockSpec(a.shape, lambda b, c: (0,) * a.ndim)
    return pl.pallas_call(
        _ssd_kernel,
        grid=(t // seq, nc),
        in_specs=[row(SSD_XBC), row(SSD_INNER), row(LANES), full(cw), full(cb), full(dtb), full(alog),
                  full(dskip), full(gn)],
        out_specs=row(SSD_INNER),
        out_shape=jax.ShapeDtypeStruct((t, SSD_INNER), BF16),
        scratch_shapes=[pltpu.VMEM((q, SSD_XBC), F32),
                        pltpu.VMEM((SSD_HEADS // 2, SSD_STATE, LANES), F32)],
        compiler_params=_cparams("arbitrary", "arbitrary"),
        name="ssd",
    )(xbc, z, dt, cw, cb, dtb, alog, dskip, gn)


def _outproj_kernel(a_ref, b_ref, x_ref, wa_ref, wb_ref, g_ref, x1_ref, hn_ref):
    mix = jnp.dot(a_ref[...], wa_ref[...], preferred_element_type=F32)
    mix = mix + jnp.dot(b_ref[...], wb_ref[...], preferred_element_type=F32)
    x1 = x_ref[...] + mix
    x1_ref[...] = x1
    hn_ref[...] = _rms(x1, g_ref[...], D_MODEL).astype(BF16)


def _outproj(a, a_blk, b, b_blk, x, w, g):
    t = x.shape[0]
    tm = 512
    half = w.shape[0] // 2
    return pl.pallas_call(
        _outproj_kernel,
        grid=(t // tm,),
        in_specs=[pl.BlockSpec((tm, half), lambda r: (r, a_blk)),
                  pl.BlockSpec((tm, half), lambda r: (r, b_blk)),
                  pl.BlockSpec((tm, D_MODEL), lambda r: (r, 0)),
                  pl.BlockSpec((half, D_MODEL), lambda r: (0, 0)),
                  pl.BlockSpec((half, D_MODEL), lambda r: (1, 0)),
                  pl.BlockSpec((1, D_MODEL), lambda r: (0, 0))],
        out_specs=[pl.BlockSpec((tm, D_MODEL), lambda r: (r, 0)),
                   pl.BlockSpec((tm, D_MODEL), lambda r: (r, 0))],
        out_shape=[jax.ShapeDtypeStruct((t, D_MODEL), F32),
                   jax.ShapeDtypeStruct((t, D_MODEL), BF16)],
        compiler_params=_cparams("arbitrary"),
        name="outproj",
    )(a, b, x, w, w, g)


def _shift_rows(a, tail, k):
    ra = pltpu.roll(a, k, 0)
    rt = pltpu.roll(tail, k, 0)
    row = lax.broadcasted_iota(jnp.int32, tail.shape, 0)
    head = jnp.where(row < k, rt, ra[:SUBLANES])
    return jnp.concatenate([head, ra[SUBLANES:]], axis=0)


def _ffn_kernel(hn_ref, x1_ref, wg_ref, wu_ref, cwg_ref, cwu_ref, cbg_ref, cbu_ref, wd_ref,
                o_ref, carry_ref, *, tiles_per_seq):
    r = pl.program_id(0)
    j = pl.program_id(1)
    hn = hn_ref[...]
    tm = hn.shape[0]
    tf = wg_ref.shape[1]

    @pl.when(r % tiles_per_seq == 0)
    def _():
        carry_ref[j] = jnp.zeros(carry_ref.shape[1:], F32)

    def branch(w_ref, cw_ref, cb_ref, lo):
        up = jnp.dot(hn, w_ref[...], preferred_element_type=F32)
        tail = carry_ref[j, :, lo:lo + tf]
        carry_ref[j, :, lo:lo + tf] = up[tm - SUBLANES:]
        cw = cw_ref[...]
        out = cb_ref[...] + cw[FFN_CONV - 1:FFN_CONV] * up
        for k in range(1, FFN_CONV):
            out = out + cw[FFN_CONV - 1 - k:FFN_CONV - k] * _shift_rows(up, tail, k)
        return out

    gate = branch(wg_ref, cwg_ref, cbg_ref, 0)
    val = branch(wu_ref, cwu_ref, cbu_ref, tf)
    act = (_silu(gate) * val).astype(BF16)
    down = jnp.dot(act, wd_ref[...], preferred_element_type=F32)

    @pl.when(j == 0)
    def _():
        o_ref[...] = x1_ref[...] + down

    @pl.when(j > 0)
    def _():
        o_ref[...] += down


def _ffn(hn, x1, w_up, cw, cb, w_down, seq):
    t = hn.shape[0]
    tm = 512
    tf = 256
    nf = D_FF // tf
    return pl.pallas_call(
        functools.partial(_ffn_kernel, tiles_per_seq=seq // tm),
        grid=(t // tm, nf),
        in_specs=[pl.BlockSpec((tm, D_MODEL), lambda r, j: (r, 0)),
                  pl.BlockSpec((tm, D_MODEL), lambda r, j: (r, 0)),
                  pl.BlockSpec((D_MODEL, tf), lambda r, j: (0, j)),
                  pl.BlockSpec((D_MODEL, tf), lambda r, j: (0, nf + j)),
                  pl.BlockSpec((FFN_CONV, tf), lambda r, j: (0, j)),
                  pl.BlockSpec((FFN_CONV, tf), lambda r, j: (0, nf + j)),
                  pl.BlockSpec((1, tf), lambda r, j: (0, j)),
                  pl.BlockSpec((1, tf), lambda r, j: (0, nf + j)),
                  pl.BlockSpec((tf, D_MODEL), lambda r, j: (j, 0))],
        out_specs=pl.BlockSpec((tm, D_MODEL), lambda r, j: (r, 0)),
        out_shape=jax.ShapeDtypeStruct((t, D_MODEL), F32),
        scratch_shapes=[pltpu.VMEM((nf, SUBLANES, 2 * tf), F32)],
        compiler_params=_cparams("arbitrary", "arbitrary"),
        name="ffn",
    )(hn, x1, w_up, w_up, cw, cw, cb, cb, w_down)


def _inproj1_kernel(x_ref, g_ref, win_ref, qag_ref, kvag_ref, wuq_ref, wukv_ref, qg_ref, kg_ref,
                    cos_ref, sin_ref, q_ref, k_ref, v_ref):
    hn = _rms(x_ref[...], g_ref[...], D_MODEL).astype(BF16)
    c = jnp.dot(hn, win_ref[...], preferred_element_type=F32)
    cq = _rms(c[:, :MLA_Q_RANK], qag_ref[...], MLA_Q_RANK).astype(BF16)
    ckv = _rms(c[:, MLA_Q_RANK:MLA_Q_RANK + MLA_KV_RANK], kvag_ref[...], MLA_KV_RANK).astype(BF16)
    k_pe = c[:, MLA_Q_RANK + MLA_KV_RANK:]
    q = jnp.dot(cq, wuq_ref[...], preferred_element_type=F32)
    kv = jnp.dot(ckv, wukv_ref[...], preferred_element_type=F32)
    cos = cos_ref[...]
    sin = sin_ref[...]
    scale = MLA_QK ** -0.5
    split = MLA_NOPE + MLA_ROPE // 2
    for h in range(MLA_HEADS):
        qh = q[:, h * LANES:(h + 1) * LANES]
        kh = kv[:, h * LANES:(h + 1) * LANES] + k_pe
        qr = _rope(_rms(qh, qg_ref[...], MLA_QK), cos, sin, split, MLA_ROPE // 2)
        kr = _rope(_rms(kh, kg_ref[...], MLA_QK), cos, sin, split, MLA_ROPE // 2)
        q_ref[:, h * LANES:(h + 1) * LANES] = (qr * scale).astype(BF16)
        k_ref[:, h * LANES:(h + 1) * LANES] = kr.astype(BF16)
    v_ref[...] = kv[:, MLA_HEADS * LANES:].astype(BF16)


def _inproj1(x, g, win, qag, kvag, wuq, wukv, qg, kg, cos, sin, seq):
    t = x.shape[0]
    tm = 256
    nblk = seq // tm
    row = lambda width: pl.BlockSpec((tm, width), lambda r: (r, 0))
    full = lambda a: pl.BlockSpec(a.shape, lambda r: (0,) * a.ndim)
    tab = pl.BlockSpec((tm, LANES), lambda r: (r % nblk, 0))
    return pl.pallas_call(
        _inproj1_kernel,
        grid=(t // tm,),
        in_specs=[row(D_MODEL), full(g), full(win), full(qag), full(kvag), full(wuq), full(wukv),
                  full(qg), full(kg), tab, tab],
        out_specs=[row(MLA_HEADS * LANES), row(MLA_HEADS * LANES), row(MLA_HEADS * MLA_V)],
        out_shape=[jax.ShapeDtypeStruct((t, MLA_HEADS * LANES), BF16),
                   jax.ShapeDtypeStruct((t, MLA_HEADS * LANES), BF16),
                   jax.ShapeDtypeStruct((t, MLA_HEADS * MLA_V), BF16)],
        compiler_params=_cparams("arbitrary"),
        name="inproj1",
    )(x, g, win, qag, kvag, wuq, wukv, qg, kg, cos, sin)


def _rope_tables(seq, dim, lane0):
    half = dim // 2
    inv_freq = 1.0 / (ROPE_THETA ** (jnp.arange(0, dim, 2, dtype=F32) / dim))
    ang = jnp.arange(seq, dtype=F32)[:, None] * inv_freq[None, :]
    cos = jnp.ones((seq, LANES), F32)
    sin = jnp.zeros((seq, LANES), F32)
    cos = cos.at[:, lane0:lane0 + dim].set(jnp.concatenate([jnp.cos(ang), jnp.cos(ang)], axis=-1))
    sin = sin.at[:, lane0:lane0 + dim].set(jnp.concatenate([-jnp.sin(ang), jnp.sin(ang)], axis=-1))
    del half
    return cos, sin


def _pad_heads(w, heads, dim):
    k = w.shape[0]
    w = w.reshape(k, heads, dim)
    return jnp.pad(w, ((0, 0), (0, 0), (0, LANES - dim))).reshape(k, heads * LANES)


def _pad_lanes(v, width=LANES):
    v = v.reshape(1, -1)
    return jnp.pad(v, ((0, 0), (0, width - v.shape[1])))


def kernel(x, mix_norm, ffn_norm, ev_w_in, ev_conv_w, ev_conv_b, ev_dt_bias, ev_a_log, ev_d_skip,
           ev_ssd_norm, ev_q_norm, ev_k_norm, ev_w_out, od_w_in, od_q_a_norm, od_w_uq, od_kv_a_norm,
           od_w_ukv, od_q_norm, od_k_norm, od_w_out, ffn_w_up, ffn_conv_w, ffn_conv_b, ffn_w_down):
    bsz, seq, _ = x.shape
    t = bsz * seq
    xf = x.reshape(t, D_MODEL)

    def mlp(hn, x1, layer):
        return _ffn(hn, x1, ffn_w_up[layer].astype(BF16), ffn_conv_w[layer],
                    ffn_conv_b[layer].reshape(1, -1), ffn_w_down[layer].astype(BF16), seq)

    w = ev_w_in[0]
    c0 = MOBA_WIDTH
    w0 = jnp.concatenate([
        _pad_heads(w[:, :c0], MOBA_HEADS, MOBA_HEAD_DIM),
        _pad_heads(w[:, c0:2 * c0], MOBA_HEADS, MOBA_HEAD_DIM),
        w[:, 2 * c0:3 * c0 + SSD_INNER + SSD_XBC],
        jnp.pad(w[:, 3 * c0 + SSD_INNER + SSD_XBC:], ((0, 0), (0, LANES - SSD_HEADS))),
    ], axis=1).astype(BF16)
    cos0, sin0 = _rope_tables(seq, MOBA_HEAD_DIM, 0)
    q, k, v, z, xbc, dt = _inproj0(xf, mix_norm[0].reshape(1, -1), w0, _pad_lanes(ev_q_norm[0]),
                                   _pad_lanes(ev_k_norm[0]), cos0, sin0, seq)
    o_attn = _attention(q.reshape(bsz, seq, -1), k.reshape(bsz, seq, -1), v.reshape(bsz, seq, -1))
    y_ssd = _ssd(xbc, z, dt, ev_conv_w[0], ev_conv_b[0].reshape(1, -1), _pad_lanes(ev_dt_bias[0]),
                 _pad_lanes(ev_a_log[0]), jnp.repeat(ev_d_skip[0], SSD_HEAD_DIM).reshape(1, -1),
                 ev_ssd_norm[0].reshape(1, -1), seq)
    x1, hn = _outproj(o_attn.reshape(t, -1), 0, y_ssd, 0, xf, ev_w_out[0].astype(BF16),
                      ffn_norm[0].reshape(1, -1))
    x2 = mlp(hn, x1, 0)

    w = od_w_in[0]
    rank = MLA_Q_RANK + MLA_KV_RANK
    win = jnp.concatenate([w[:, :rank], jnp.zeros((D_MODEL, MLA_NOPE), F32), w[:, rank:],
                           jnp.zeros((D_MODEL, LANES - MLA_NOPE - MLA_ROPE), F32)], axis=1).astype(BF16)
    wuq = _pad_heads(od_w_uq[0], MLA_HEADS, MLA_QK).astype(BF16)
    wkv = od_w_ukv[0].reshape(MLA_KV_RANK, MLA_HEADS, MLA_NOPE + MLA_V)
    wukv = jnp.concatenate([
        _pad_heads(wkv[:, :, :MLA_NOPE].reshape(MLA_KV_RANK, -1), MLA_HEADS, MLA_NOPE),
        wkv[:, :, MLA_NOPE:].reshape(MLA_KV_RANK, -1)], axis=1).astype(BF16)
    cos1, sin1 = _rope_tables(seq, MLA_ROPE, MLA_NOPE)
    q, k, v = _inproj1(x2, mix_norm[1].reshape(1, -1), win, od_q_a_norm[0].reshape(1, -1),
                       od_kv_a_norm[0].reshape(1, -1), wuq, wukv, _pad_lanes(od_q_norm[0]),
                       _pad_lanes(od_k_norm[0]), cos1, sin1, seq)
    o = _attention(q.reshape(bsz, seq, -1), k.reshape(bsz, seq, -1), v.reshape(bsz, seq, -1))
    o = o.reshape(t, -1)
    x3, hn = _outproj(o, 0, o, 1, x2, od_w_out[0].astype(BF16), ffn_norm[1].reshape(1, -1))
    x4 = mlp(hn, x3, 1)
    return x4.reshape(bsz, seq, D_MODEL)
```
